```python
import math
import jax, jax.numpy as jnp
from jax import lax
import numpy as np

D_MODEL = 2048
BATCH = 2
SEQ = 16384
DEPTH = 1
DEC_BATCH = 2
DEC_SEQ = 4096
PAST_LEN = 128

HEAD_DIM = 128
MIX_WIDTH = D_MODEL
A_HEADS = 8
A_KV_HEADS = 2
A_RADIUS = 128
B_HEADS = 8
DILATIONS = ((128, 1), (512, 4), (2048, 16))
N_DIL = len(DILATIONS)
B_RADIUS = 64
NUM_BUCKETS = 32
REL_MAX_DISTANCE = 1024
N_BIAS_HEADS = A_HEADS + N_DIL * B_HEADS
D_FF = 4 * D_MODEL
A_Q_COLS = A_HEADS * HEAD_DIM
A_KV_COLS = A_KV_HEADS * HEAD_DIM
B_COLS = N_DIL * B_HEADS * HEAD_DIM
PROJ_COLS = A_Q_COLS + 2 * A_KV_COLS + 3 * B_COLS
EPS = 1e-6
NEG_INF = -1e30

kernel_name = "hybrid_window_dilated_encoder"


def rms_norm(x, g):
    xf = x.astype(jnp.float32)
    y = xf * lax.rsqrt(jnp.mean(xf * xf, axis=-1, keepdims=True) + EPS)
    return (y * g.astype(jnp.float32)).astype(x.dtype)


def rel_bucket(rel):
    half = NUM_BUCKETS // 2
    exact = half // 2
    n = jnp.abs(rel)
    large = exact + (jnp.log(jnp.maximum(n, 1).astype(jnp.float32) / exact)
                     / math.log(REL_MAX_DISTANCE / exact) * (half - exact)).astype(jnp.int32)
    large = jnp.minimum(large, half - 1)
    return jnp.where(rel > 0, half, 0) + jnp.where(n < exact, n, large)


def to_strided(t, r):
    B, S = t.shape[:2]
    t = t.reshape(B, S // r, r, *t.shape[2:])
    return jnp.moveaxis(t, 2, 1).reshape(B * r, S // r, *t.shape[3:])


def from_strided(t, B, r):
    L = t.shape[1]
    t = t.reshape(B, r, L, *t.shape[2:])
    return jnp.moveaxis(t, 1, 2).reshape(B, L * r, *t.shape[3:])


def banded_attention(q, k, v, bias_cols, radius, dilation, sink=None):
    B, S, KV, G, D = q.shape
    r, blk = dilation, radius
    L = S // r
    nb = -(-L // blk)
    Lp = nb * blk
    q = to_strided(q, r)
    k = to_strided(k, r)
    v = to_strided(v, r)
    N = q.shape[0]
    q = jnp.pad(q, ((0, 0), (0, Lp - L), (0, 0), (0, 0), (0, 0)))
    kpad = ((0, 0), (blk, blk + Lp - L), (0, 0), (0, 0))
    k = jnp.pad(k, kpad)
    v = jnp.pad(v, kpad)
    qb = q.reshape(N, nb, blk, KV, G, D)

    def windows(t):
        t = t.reshape(N, nb + 2, blk, KV, D)
        return jnp.concatenate([t[:, :-2], t[:, 1:-1], t[:, 2:]], axis=2)

    kw, vw = windows(k), windows(v)
    delta = jnp.arange(3 * blk)[None, :] - blk - jnp.arange(blk)[:, None]
    band = jnp.abs(delta) <= radius
    kpos = jnp.arange(nb)[:, None] * blk + jnp.arange(3 * blk)[None, :] - blk
    valid = (kpos >= 0) & (kpos < L)
    mask = band[None] & valid[:, None, :]
    bias = bias_cols[rel_bucket(delta * r)]
    bias = bias.reshape(blk, 3 * blk, KV, G).transpose(2, 3, 0, 1).astype(jnp.float32)
    logits = jnp.einsum('nbqkgd,nbskd->nbkgqs', qb.astype(jnp.float32), kw.astype(jnp.float32)) * (D ** -0.5) + bias
    logits = jnp.where(mask[None, :, None, None], logits, NEG_INF)
    m = jnp.max(logits, axis=-1, keepdims=True)
    if sink is not None:
        s = sink.astype(jnp.float32)[:, :, None, None]
        m = jnp.maximum(m, s)
    p = jnp.exp(logits - m)
    denom = jnp.sum(p, axis=-1, keepdims=True)
    if sink is not None:
        denom = denom + jnp.exp(s - m)
    out = jnp.einsum('nbkgqs,nbskd->nbkgqd', p, vw.astype(jnp.float32)) / denom
    lse = (m + jnp.log(denom))[..., 0]
    out = out.transpose(0, 1, 4, 2, 3, 5).reshape(N, Lp, KV, G, D)[:, :L]
    lse = lse.transpose(0, 1, 4, 2, 3).reshape(N, Lp, KV, G)[:, :L]
    return from_strided(out, B, r), from_strided(lse, B, r)


def encoder_layer(x, c, w_ada, b_ada, norm1_g, w_in, q_norm_a, k_norm_a, sink_a,
                  q_norm_b, k_norm_b, rel_bias, w_out, norm2_g, w1, w2):
    B, S, _ = x.shape
    mod = (jax.nn.silu(c) @ w_ada + b_ada).reshape(B, 6, D_MODEL)[:, :, None, :]
    shift1, scale1, gate1, shift2, scale2, gate2 = [mod[:, i] for i in range(6)]

    h = rms_norm(x, norm1_g) * (1 + scale1) + shift1
    proj = h @ w_in
    cuts = np.cumsum([A_Q_COLS, A_KV_COLS, A_KV_COLS, B_COLS, B_COLS])[:].tolist()
    qa, ka, va, qb, kb, vb = jnp.split(proj, cuts, axis=-1)

    qa = rms_norm(qa.reshape(B, S, A_HEADS, HEAD_DIM), q_norm_a)
    qa = qa.reshape(B, S, A_KV_HEADS, A_HEADS // A_KV_HEADS, HEAD_DIM)
    ka = rms_norm(ka.reshape(B, S, A_KV_HEADS, HEAD_DIM), k_norm_a)
    va = va.reshape(B, S, A_KV_HEADS, HEAD_DIM)
    ya, _ = banded_attention(qa, ka, va, rel_bias[:, :A_HEADS], A_RADIUS, 1,
                             sink=sink_a.reshape(A_KV_HEADS, A_HEADS // A_KV_HEADS))
    ya = ya.reshape(B, S, A_HEADS * HEAD_DIM)

    qb = qb.reshape(B, S, N_DIL, B_HEADS, HEAD_DIM)
    kb = kb.reshape(B, S, N_DIL, B_HEADS, HEAD_DIM)
    vb = vb.reshape(B, S, N_DIL, B_HEADS, HEAD_DIM)
    outs, lses = [], []
    for g, (_, r) in enumerate(DILATIONS):
        qg = rms_norm(qb[:, :, g], q_norm_b[g])[:, :, :, None, :]
        kg = rms_norm(kb[:, :, g], k_norm_b[g])
        cols = rel_bias[:, A_HEADS + g * B_HEADS: A_HEADS + (g + 1) * B_HEADS]
        o, lse = banded_attention(qg, kg, vb[:, :, g], cols, B_RADIUS, r)
        outs.append(o)
        lses.append(lse)
    alpha = jax.nn.softmax(jnp.stack(lses, axis=0), axis=0)
    yb = jnp.sum(alpha[..., None] * jnp.stack(outs, axis=0), axis=0).reshape(B, S, B_HEADS * HEAD_DIM)

    y = jnp.concatenate([ya, yb], axis=-1).astype(x.dtype) @ w_out
    x = x + gate1 * y

    h2 = rms_norm(x, norm2_g) * (1 + scale2) + shift2
    f = jnp.square(jax.nn.relu(h2 @ w1)) @ w2
    return x + gate2 * f


def trunk(x, c, w_ada, b_ada, norm1_g, w_in, q_norm_a, k_norm_a, sink_a,
          q_norm_b, k_norm_b, rel_bias, w_out, norm2_g, w1, w2):
    for l in range(DEPTH):
        x = encoder_layer(x, c, w_ada[l], b_ada[l], norm1_g[l], w_in[l], q_norm_a[l], k_norm_a[l],
                          sink_a[l], q_norm_b[l], k_norm_b[l], rel_bias, w_out[l], norm2_g[l],
                          w1[l], w2[l])
    return x


def setup_inputs(seed: int = 0) -> dict:
    key = jax.random.key(seed)
    ks = jax.random.split(key, 20)
    f32 = jnp.float32
    nrm = lambda k, shape, s: jax.random.normal(k, shape, f32) * s
    return {
        "x_prompt": nrm(ks[0], (BATCH, SEQ, D_MODEL), 1.0),
        "x_sample": nrm(ks[1], (DEC_BATCH, DEC_SEQ, D_MODEL), 1.0),
        "c_prompt": nrm(ks[2], (BATCH, D_MODEL), 1.0),
        "c_sample": nrm(ks[3], (DEC_BATCH, D_MODEL), 1.0),
        "w_ada": nrm(ks[4], (DEPTH, D_MODEL, 6 * D_MODEL), 0.5 * D_MODEL ** -0.5),
        "b_ada": nrm(ks[5], (DEPTH, 6 * D_MODEL), 0.02),
        "norm1_g": 1.0 + nrm(ks[6], (DEPTH, D_MODEL), 0.02),
        "w_in": nrm(ks[7], (DEPTH, D_MODEL, PROJ_COLS), D_MODEL ** -0.5),
        "q_norm_a": 1.0 + nrm(ks[8], (DEPTH, HEAD_DIM), 0.02),
        "k_norm_a": 1.0 + nrm(ks[9], (DEPTH, HEAD_DIM), 0.02),
        "sink_a": nrm(ks[10], (DEPTH, A_HEADS), 1.0),
        "q_norm_b": 1.0 + nrm(ks[11], (DEPTH, N_DIL, HEAD_DIM), 0.02),
        "k_norm_b": 1.0 + nrm(ks[12], (DEPTH, N_DIL, HEAD_DIM), 0.02),
        "rel_bias": nrm(ks[13], (NUM_BUCKETS, N_BIAS_HEADS), 0.5),
        "w_out": nrm(ks[14], (DEPTH, MIX_WIDTH, D_MODEL), MIX_WIDTH ** -0.5),
        "norm2_g": 1.0 + nrm(ks[15], (DEPTH, D_MODEL), 0.02),
        "w1": nrm(ks[16], (DEPTH, D_MODEL, D_FF), D_MODEL ** -0.5),
        "w2": nrm(ks[17], (DEPTH, D_FF, D_MODEL), D_FF ** -0.5),
    }


def reference(x_prompt, x_sample, c_prompt, c_sample, w_ada, b_ada, norm1_g, w_in, q_norm_a,
              k_norm_a, sink_a, q_norm_b, k_norm_b, rel_bias, w_out, norm2_g, w1, w2):
    y_prompt = trunk(x_prompt, c_prompt, w_ada, b_ada, norm1_g, w_in, q_norm_a, k_norm_a, sink_a,
                     q_norm_b, k_norm_b, rel_bias, w_out, norm2_g, w1, w2)
    y_sample = trunk(x_sample, c_sample, w_ada, b_ada, norm1_g, w_in, q_norm_a, k_norm_a, sink_a,
                     q_norm_b, k_norm_b, rel_bias, w_out, norm2_g, w1, w2)
    return (y_prompt, y_sample)
```

```python
import functools
import math

import numpy as np
import jax
import jax.numpy as jnp
from jax import lax
from jax.experimental import pallas as pl
from jax.experimental.pallas import tpu as pltpu

D_MODEL = 2048
HEAD_DIM = 128
A_HEADS = 8
A_KV_HEADS = 2
A_GROUP = A_HEADS // A_KV_HEADS
A_RADIUS = 128
B_HEADS = 8
B_RADIUS = 64
DILATIONS = (1, 4, 16)
NUM_BUCKETS = 32
REL_MAX_DISTANCE = 1024
D_FF = 4 * D_MODEL
A_Q_COLS = A_HEADS * HEAD_DIM
A_KV_COLS = A_KV_HEADS * HEAD_DIM
A_COLS = A_Q_COLS + 2 * A_KV_COLS
B_GROUP_COLS = B_HEADS * HEAD_DIM
B_QKV_COLS = 3 * B_GROUP_COLS
PROJ_COLS = A_COLS + len(DILATIONS) * B_QKV_COLS
EPS = 1e-6
NEG_INF = -1e30
SCALE = HEAD_DIM ** -0.5

ADA_ROWS = 16
ADA_TN = 1024
PROJ_TN = 512
A_TQ = 128
B_TQ = 128
LSE_LANES = 128
FFN_TF = 1024
VMEM_LIMIT = 56 * 1024 * 1024


def _cparams(n_axes):
    return pltpu.CompilerParams(
        dimension_semantics=("arbitrary",) * n_axes, vmem_limit_bytes=VMEM_LIMIT)


def _ada_kernel(c_ref, w_ref, b_ref, o_ref):
    c = c_ref[...]
    s = c * (1.0 / (1.0 + jnp.exp(-c)))
    acc = jnp.dot(s.astype(jnp.bfloat16), w_ref[...].astype(jnp.bfloat16),
                  preferred_element_type=jnp.float32)
    o_ref[...] = acc + b_ref[...]


def _ada(c_pad, w_ada, b_ada):
    n = w_ada.shape[1]
    return pl.pallas_call(
        _ada_kernel,
        out_shape=jax.ShapeDtypeStruct((ADA_ROWS, n), jnp.float32),
        grid=(n // ADA_TN,),
        in_specs=[pl.BlockSpec((ADA_ROWS, D_MODEL), lambda j: (0, 0)),
                  pl.BlockSpec((D_MODEL, ADA_TN), lambda j: (0, j)),
                  pl.BlockSpec((1, ADA_TN), lambda j: (0, j))],
        out_specs=pl.BlockSpec((ADA_ROWS, ADA_TN), lambda j: (0, j)),
        compiler_params=_cparams(1),
        name="ada",
    )(c_pad, w_ada, b_ada)


def _rel_bucket_np(rel):
    half = NUM_BUCKETS // 2
    exact = half // 2
    n = np.abs(rel)
    large = exact + (np.log(np.maximum(n, 1).astype(np.float32) / np.float32(exact))
                     / np.float32(math.log(REL_MAX_DISTANCE / exact)) * (half - exact)).astype(np.int32)
    large = np.minimum(large, half - 1)
    return np.where(rel > 0, half, 0) + np.where(n < exact, n, large)


def _bias_codes(tq, halo, radius, dilation):
    tk = tq + 2 * halo
    delta = np.arange(tk)[None, :] - halo - np.arange(tq)[:, None]
    code = np.where(np.abs(delta) <= radius, _rel_bucket_np(delta * dilation), -1).astype(np.int32)
    col = np.arange(tk)[None, :]
    out = []
    for v in range(4):
        c = code
        if v & 1:
            c = np.where(col < halo, -1, c)
        if v & 2:
            c = np.where(col >= halo + tq, -1, c)
        out.append(c)
    return np.stack(out).astype(np.int32)


def _bias_kernel(rb_ref, code_ref, o_ref, *, head0):
    h = pl.program_id(1) + head0
    code = code_ref[0]
    acc = jnp.full(code.shape, NEG_INF, jnp.float32)
    for b in range(NUM_BUCKETS):
        acc = jnp.where(code == b, rb_ref[b, h], acc)
    o_ref[0, 0] = acc


def _bias_tiles(rel_bias, codes, head0, n_heads):
    _, tq, tk = codes.shape
    return pl.pallas_call(
        functools.partial(_bias_kernel, head0=head0),
        out_shape=jax.ShapeDtypeStruct((4, n_heads, tq, tk), jnp.float32),
        grid=(4, n_heads),
        in_specs=[pl.BlockSpec(memory_space=pltpu.SMEM),
                  pl.BlockSpec((1, tq, tk), lambda v, h: (v, 0, 0))],
        out_specs=pl.BlockSpec((1, 1, tq, tk), lambda v, h: (v, h, 0, 0)),
        compiler_params=_cparams(2),
        name="bias_tiles",
    )(rel_bias, codes)


def _proj_tiles():
    na = A_COLS // PROJ_TN
    nb = B_QKV_COLS // PROJ_TN
    starts = [0, na, na + nb, na + 2 * nb]
    return na, nb, starts


def _proj_kernel(x_ref, mod_ref, g1_ref, w_ref, gain_ref, flag_ref,
                 oa_ref, ob0_ref, ob1_ref, ob2_ref, hf_ref, h1_ref, h4_ref, h16_ref, *, tm):
    j = pl.program_id(2)
    na, nb, starts = _proj_tiles()

    @pl.when(j == 0)
    def _():
        x = x_ref[0]
        inv = lax.rsqrt(jnp.mean(x * x, axis=-1, keepdims=True) + EPS)
        for k in range(D_MODEL // HEAD_DIM):
            ks = slice(k * HEAD_DIM, (k + 1) * HEAD_DIM)
            h = (x_ref[0, :, ks] * inv) * g1_ref[:, ks]
            h = h * (1.0 + mod_ref[0, 1:2, ks]) + mod_ref[0, 0:1, ks]
            h1_ref[:, ks] = h.astype(jnp.bfloat16)
            hf_ref[...] = h
            for r, hr_ref in ((4, h4_ref), (16, h16_ref)):
                n = tm // r
                for c in range(r):
                    hr_ref[c * n:(c + 1) * n, ks] = hf_ref[pl.ds(c, n, stride=r), :].astype(jnp.bfloat16)

    def emit(h_ref, o_ref, r):
        acc = jnp.dot(h_ref[...], w_ref[...], preferred_element_type=jnp.float32)
        for t in range(PROJ_TN // HEAD_DIM):
            sl = slice(t * HEAD_DIM, (t + 1) * HEAD_DIM)
            a = acc[:, sl]
            inv = lax.rsqrt(jnp.mean(a * a, axis=-1, keepdims=True) + EPS)
            normed = (a * inv) * gain_ref[:, sl]
            res = jnp.where(flag_ref[:, sl] > 0.0, normed, a).astype(jnp.bfloat16)
            if r is None:
                o_ref[0, :, sl] = res
            else:
                n = tm // r
                for c in range(r):
                    o_ref[0, c, :, sl] = res[c * n:(c + 1) * n, :]

    @pl.when(j < starts[1])
    def _():
        emit(h1_ref, oa_ref, None)

    @pl.when((j >= starts[1]) & (j < starts[2]))
    def _():
        emit(h1_ref, ob0_ref, 1)

    @pl.when((j >= starts[2]) & (j < starts[3]))
    def _():
        emit(h4_ref, ob1_ref, 4)

    @pl.when(j >= starts[3])
    def _():
        emit(h16_ref, ob2_ref, 16)


def _proj(x, mod, norm1_g, w_in_p, gain_cols, flag_cols, tm):
    B, S, _ = x.shape
    na, nb, starts = _proj_tiles()
    n_tiles = na + 3 * nb
    tn = PROJ_TN

    def b_spec(g, r):
        s0 = starts[g + 1]
        return pl.BlockSpec((1, r, tm // r, tn),
                            lambda b, i, j: (b, 0, i, jnp.clip(j - s0, 0, nb - 1)))

    out_shapes = [jax.ShapeDtypeStruct((B, S, A_COLS), jnp.bfloat16)]
    out_specs = [pl.BlockSpec((1, tm, tn), lambda b, i, j: (b, i, jnp.minimum(j, na - 1)))]
    for g, r in enumerate(DILATIONS):
        out_shapes.append(jax.ShapeDtypeStruct((B, r, S // r, B_QKV_COLS), jnp.bfloat16))
        out_specs.append(b_spec(g, r))
    return pl.pallas_call(
        functools.partial(_proj_kernel, tm=tm),
        out_shape=out_shapes,
        grid=(B, S // tm, n_tiles),
        in_specs=[pl.BlockSpec((1, tm, D_MODEL), lambda b, i, j: (b, i, 0)),
                  pl.BlockSpec((1, 6, D_MODEL), lambda b, i, j: (b, 0, 0)),
                  pl.BlockSpec((1, D_MODEL), lambda b, i, j: (0, 0)),
                  pl.BlockSpec((D_MODEL, tn), lambda b, i, j: (0, j)),
                  pl.BlockSpec((1, tn), lambda b, i, j: (0, j)),
                  pl.BlockSpec((1, tn), lambda b, i, j: (0, j))],
        out_specs=out_specs,
        scratch_shapes=[pltpu.VMEM((tm, HEAD_DIM), jnp.float32),
                        pltpu.VMEM((tm, D_MODEL), jnp.bfloat16),
                        pltpu.VMEM((tm, D_MODEL), jnp.bfloat16),
                        pltpu.VMEM((tm, D_MODEL), jnp.bfloat16)],
        compiler_params=_cparams(3),
        name="proj",
    )(x, mod, norm1_g, w_in_p, gain_cols, flag_cols)


def _attn_a_kernel(sink_ref, q_ref, kp_ref, kc_ref, kn_ref, vp_ref, vc_ref, vn_ref, bias_ref, o_ref):
    tq = A_TQ
    for kh in range(A_KV_HEADS):
        ks = slice(kh * HEAD_DIM, (kh + 1) * HEAD_DIM)
        k = jnp.concatenate([kp_ref[0, :, ks], kc_ref[0, :, ks], kn_ref[0, :, ks]], axis=0)
        v = jnp.concatenate([vp_ref[0, :, ks], vc_ref[0, :, ks], vn_ref[0, :, ks]], axis=0)
        q = jnp.concatenate(
            [q_ref[0, :, (kh * A_GROUP + g) * HEAD_DIM:(kh * A_GROUP + g + 1) * HEAD_DIM]
             for g in range(A_GROUP)], axis=0)
        s = lax.dot_general(q, k, (((1,), (1,)), ((), ())), preferred_element_type=jnp.float32)
        ps, ls = [], []
        for g in range(A_GROUP):
            h = kh * A_GROUP + g
            sg = s[g * tq:(g + 1) * tq, :] * SCALE + bias_ref[0, h]
            sink = sink_ref[h]
            m = jnp.maximum(jnp.max(sg, axis=-1, keepdims=True), sink)
            p = jnp.exp(sg - m)
            ls.append(jnp.sum(p, axis=-1, keepdims=True) + jnp.exp(sink - m))
            ps.append(p.astype(jnp.bfloat16))
        o = jnp.dot(jnp.concatenate(ps, axis=0), v, preferred_element_type=jnp.float32)
        for g in range(A_GROUP):
            h = kh * A_GROUP + g
            o_ref[0, :, h * HEAD_DIM:(h + 1) * HEAD_DIM] = (
                o[g * tq:(g + 1) * tq, :] / ls[g]).astype(jnp.bfloat16)


def _edge_variant(i, n):
    return (i == 0).astype(jnp.int32) + 2 * (i == n - 1).astype(jnp.int32)


def _attn_a(proj_a, sink, bias_a):
    B, S, _ = proj_a.shape
    tq = A_TQ
    n = S // tq
    kcol = A_Q_COLS // A_KV_COLS
    vcol = kcol + 1

    def kv(col, off):
        return pl.BlockSpec((1, tq, A_KV_COLS),
                            lambda b, i: (b, jnp.clip(i + off, 0, n - 1), col))

    return pl.pallas_call(
        _attn_a_kernel,
        out_shape=jax.ShapeDtypeStruct((B, S, A_Q_COLS), jnp.bfloat16),
        grid=(B, n),
        in_specs=[pl.BlockSpec(memory_space=pltpu.SMEM),
                  pl.BlockSpec((1, tq, A_Q_COLS), lambda b, i: (b, i, 0)),
                  kv(kcol, -1), kv(kcol, 0), kv(kcol, 1),
                  kv(vcol, -1), kv(vcol, 0), kv(vcol, 1),
                  pl.BlockSpec((1, A_HEADS, tq, 3 * tq),
                               lambda b, i: (_edge_variant(i, n), 0, 0, 0))],
        out_specs=pl.BlockSpec((1, tq, A_Q_COLS), lambda b, i: (b, i, 0)),
        compiler_params=_cparams(2),
        name="attn_a",
    )(sink, proj_a, proj_a, proj_a, proj_a, proj_a, proj_a, proj_a, bias_a)


def _attn_b_kernel(q_ref, kp_ref, kc_ref, kn_ref, vp_ref, vc_ref, vn_ref, bias_ref, o_ref, lse_ref):
    lane = lax.broadcasted_iota(jnp.int32, (B_TQ, LSE_LANES), 1)
    lse_tile = jnp.zeros((B_TQ, LSE_LANES), jnp.float32)
    for h in range(B_HEADS):
        hs = slice(h * HEAD_DIM, (h + 1) * HEAD_DIM)
        k = jnp.concatenate([kp_ref[0, :, hs], kc_ref[0, :, hs], kn_ref[0, :, hs]], axis=0)
        v = jnp.concatenate([vp_ref[0, :, hs], vc_ref[0, :, hs], vn_ref[0, :, hs]], axis=0)
        s = lax.dot_general(q_ref[0, :, hs], k, (((1,), (1,)), ((), ())),
                            preferred_element_type=jnp.float32)
        s = s * SCALE + bias_ref[0, h]
        m = jnp.max(s, axis=-1, keepdims=True)
        p = jnp.exp(s - m)
        l = jnp.sum(p, axis=-1, keepdims=True)
        o = jnp.dot(p.astype(jnp.bfloat16), v, preferred_element_type=jnp.float32)
        o_ref[0, :, hs] = o / l
        lse_tile = jnp.where(lane == h, m + jnp.log(l), lse_tile)
    lse_ref[0] = lse_tile


def _attn_b(proj_g, bias_g):
    N, L, _ = proj_g.shape
    tq = B_TQ
    halo = B_RADIUS
    n = L // tq
    ratio = tq // halo
    nh = L // halo

    def halo_spec(col, after):
        if after:
            return pl.BlockSpec((1, halo, B_GROUP_COLS),
                                lambda s, i: (s, jnp.minimum((i + 1) * ratio, nh - 1), col))
        return pl.BlockSpec((1, halo, B_GROUP_COLS),
                            lambda s, i: (s, jnp.maximum(i * ratio - 1, 0), col))

    def cur_spec(col):
        return pl.BlockSpec((1, tq, B_GROUP_COLS), lambda s, i: (s, i, col))

    return pl.pallas_call(
        _attn_b_kernel,
        out_shape=[jax.ShapeDtypeStruct((N, L, B_GROUP_COLS), jnp.float32),
                   jax.ShapeDtypeStruct((N, L, LSE_LANES), jnp.float32)],
        grid=(N, n),
        in_specs=[cur_spec(0),
                  halo_spec(1, False), cur_spec(1), halo_spec(1, True),
                  halo_spec(2, False), cur_spec(2), halo_spec(2, True),
                  pl.BlockSpec((1, B_HEADS, tq, tq + 2 * halo),
                               lambda s, i: (_edge_variant(i, n), 0, 0, 0))],
        out_specs=[pl.BlockSpec((1, tq, B_GROUP_COLS), lambda s, i: (s, i, 0)),
                   pl.BlockSpec((1, tq, LSE_LANES), lambda s, i: (s, i, 0))],
        compiler_params=_cparams(2),
        name="attn_b",
    )(proj_g, proj_g, proj_g, proj_g, proj_g, proj_g, proj_g, bias_g)


def _mix_kernel(x_ref, mod_ref, g2_ref, ya_ref, o0_ref, o1_ref, o2_ref, l0_ref, l1_ref, l2_ref, w_ref,
                x1_ref, h2_ref, os1_ref, os2_ref, ls1_ref, ls2_ref, y_ref, *, tm):
    for r, o_ref, l_ref, os_ref, ls_ref in ((4, o1_ref, l1_ref, os1_ref, ls1_ref),
                                            (16, o2_ref, l2_ref, os2_ref, ls2_ref)):
        n = tm // r
        for c in range(r):
            ls_ref[pl.ds(c, n, stride=r), :] = l_ref[0, c]
            for h in range(B_HEADS):
                os_ref[h, pl.ds(c, n, stride=r), :] = o_ref[0, c, :, h * HEAD_DIM:(h + 1) * HEAD_DIM]
    lses = (l0_ref[0, 0], ls1_ref[...], ls2_ref[...])
    m = jnp.maximum(jnp.maximum(lses[0], lses[1]), lses[2])
    es = [jnp.exp(l - m) for l in lses]
    tot = es[0] + es[1] + es[2]
    alphas = [e / tot for e in es]
    y_ref[:, 0:A_Q_COLS] = ya_ref[0]
    for h in range(B_HEADS):
        hs = slice(h * HEAD_DIM, (h + 1) * HEAD_DIM)
        yb = (alphas[0][:, h:h + 1] * o0_ref[0, 0, :, hs]
              + alphas[1][:, h:h + 1] * os1_ref[h]
              + alphas[2][:, h:h + 1] * os2_ref[h])
        y_ref[:, A_Q_COLS + h * HEAD_DIM:A_Q_COLS + (h + 1) * HEAD_DIM] = yb.astype(jnp.bfloat16)
    y = jnp.dot(y_ref[...], w_ref[...], preferred_element_type=jnp.float32)
    x1 = x_ref[0] + mod_ref[0, 2:3, :] * y
    x1_ref[0] = x1
    inv = lax.rsqrt(jnp.mean(x1 * x1, axis=-1, keepdims=True) + EPS)
    h2 = ((x1 * inv) * g2_ref[...]) * (1.0 + mod_ref[0, 4:5, :]) + mod_ref[0, 3:4, :]
    h2_ref[0] = h2.astype(jnp.bfloat16)


def _mix(x, mod, norm2_g, ya, outs, lses, w_out_b, tm):
    B, S, _ = x.shape

    def strided_spec(r, cols):
        return pl.BlockSpec((1, r, tm // r, cols), lambda b, i: (b, 0, i, 0))

    in_specs = [pl.BlockSpec((1, tm, D_MODEL), lambda b, i: (b, i, 0)),
                pl.BlockSpec((1, 6, D_MODEL), lambda b, i: (b, 0, 0)),
                pl.BlockSpec((1, D_MODEL), lambda b, i: (0, 0)),
                pl.BlockSpec((1, tm, A_Q_COLS), lambda b, i: (b, i, 0))]
    in_specs += [strided_spec(r, B_GROUP_COLS) for r in DILATIONS]
    in_specs += [strided_spec(r, LSE_LANES) for r in DILATIONS]
    in_specs += [pl.BlockSpec((D_MODEL, D_MODEL), lambda b, i: (0, 0))]
    return pl.pallas_call(
        functools.partial(_mix_kernel, tm=tm),
        out_shape=[jax.ShapeDtypeStruct((B, S, D_MODEL), jnp.float32),
                   jax.ShapeDtypeStruct((B, S, D_MODEL), jnp.bfloat16)],
        grid=(B, S // tm),
        in_specs=in_specs,
        out_specs=[pl.BlockSpec((1, tm, D_MODEL), lambda b, i: (b, i, 0)),
                   pl.BlockSpec((1, tm, D_MODEL), lambda b, i: (b, i, 0))],
        scratch_shapes=[pltpu.VMEM((B_HEADS, tm, HEAD_DIM), jnp.float32),
                        pltpu.VMEM((B_HEADS, tm, HEAD_DIM), jnp.float32),
                        pltpu.VMEM((tm, LSE_LANES), jnp.float32),
                        pltpu.VMEM((tm, LSE_LANES), jnp.float32),
                        pltpu.VMEM((tm, D_MODEL), jnp.bfloat16)],
        compiler_params=_cparams(2),
        name="mix",
    )(x, mod, norm2_g, ya, *outs, *lses, w_out_b)


def _ffn_kernel(x1_ref, h2_ref, mod_ref, w1_ref, w2_ref, o_ref, acc_ref):
    f = pl.program_id(2)

    @pl.when(f == 0)
    def _():
        acc_ref[...] = jnp.zeros_like(acc_ref)

    u = jnp.dot(h2_ref[0], w1_ref[...], preferred_element_type=jnp.float32)
    u = jnp.maximum(u, 0.0)
    u = (u * u).astype(jnp.bfloat16)
    acc_ref[...] += jnp.dot(u, w2_ref[...], preferred_element_type=jnp.float32)

    @pl.when(f == pl.num_programs(2) - 1)
    def _():
        o_ref[0] = x1_ref[0] + mod_ref[0, 5:6, :] * acc_ref[...]


def _ffn(x1, h2, mod, w1_b, w2_b, tm):
    B, S, _ = x1.shape
    tf = FFN_TF
    return pl.pallas_call(
        _ffn_kernel,
        out_shape=jax.ShapeDtypeStruct((B, S, D_MODEL), jnp.float32),
        grid=(B, S // tm, D_FF // tf),
        in_specs=[pl.BlockSpec((1, tm, D_MODEL), lambda b, i, f: (b, i, 0)),
                  pl.BlockSpec((1, tm, D_MODEL), lambda b, i, f: (b, i, 0)),
                  pl.BlockSpec((1, 6, D_MODEL), lambda b, i, f: (b, 0, 0)),
                  pl.BlockSpec((D_MODEL, tf), lambda b, i, f: (0, f)),
                  pl.BlockSpec((tf, D_MODEL), lambda b, i, f: (f, 0))],
        out_specs=pl.BlockSpec((1, tm, D_MODEL), lambda b, i, f: (b, i, 0)),
        scratch_shapes=[pltpu.VMEM((tm, D_MODEL), jnp.float32)],
        compiler_params=_cparams(3),
        name="ffn",
    )(x1, h2, mod, w1_b, w2_b)


def _prep_weights(w_in, q_norm_a, k_norm_a, q_norm_b, k_norm_b, w_out, w1, w2):
    qb0 = A_COLS
    kb0 = qb0 + len(DILATIONS) * B_GROUP_COLS
    vb0 = kb0 + len(DILATIONS) * B_GROUP_COLS
    parts = [w_in[:, :A_COLS]]
    ones_h = jnp.ones((HEAD_DIM,), jnp.float32)
    gains = [jnp.tile(q_norm_a, A_HEADS), jnp.tile(k_norm_a, A_KV_HEADS), jnp.tile(ones_h, A_KV_HEADS)]
    flags = [np.ones(A_Q_COLS + A_KV_COLS, np.float32), np.zeros(A_KV_COLS, np.float32)]
    for g in range(len(DILATIONS)):
        for base in (qb0, kb0, vb0):
            parts.append(w_in[:, base + g * B_GROUP_COLS: base + (g + 1) * B_GROUP_COLS])
        gains += [jnp.tile(q_norm_b[g], B_HEADS), jnp.tile(k_norm_b[g], B_HEADS), jnp.tile(ones_h, B_HEADS)]
        flags += [np.ones(2 * B_GROUP_COLS, np.float32), np.zeros(B_GROUP_COLS, np.float32)]
    w_in_p = jnp.concatenate(parts, axis=1).astype(jnp.bfloat16)
    gain_cols = jnp.concatenate(gains)[None, :].astype(jnp.float32)
    flag_cols = jnp.asarray(np.concatenate(flags))[None, :]
    return (w_in_p, gain_cols, flag_cols, w_out.astype(jnp.bfloat16),
            w1.astype(jnp.bfloat16), w2.astype(jnp.bfloat16))


def _pick_tile(S, want, mult):
    t = min(want, S)
    while S % t or t % mult:
        t //= 2
    return t


def _layer(x, mod, norm1_g, norm2_g, sink, wts, bias_a, bias_b):
    w_in_p, gain_cols, flag_cols, w_out_b, w1_b, w2_b = wts
    B, S, _ = x.shape
    proj_a, *proj_b = _proj(x, mod, norm1_g, w_in_p, gain_cols, flag_cols, _pick_tile(S, 1024, 256))
    ya = _attn_a(proj_a, sink, bias_a)
    outs, lses = [], []
    for g, r in enumerate(DILATIONS):
        L = S // r
        o, lse = _attn_b(proj_b[g].reshape(B * r, L, B_QKV_COLS), bias_b[g])
        outs.append(o.reshape(B, r, L, B_GROUP_COLS))
        lses.append(lse.reshape(B, r, L, LSE_LANES))
    x1, h2 = _mix(x, mod, norm2_g, ya, outs, lses, w_out_b, _pick_tile(S, 256, 128))
    return _ffn(x1, h2, mod, w1_b, w2_b, _pick_tile(S, 512, 128))


def kernel(x_prompt, x_sample, c_prompt, c_sample, w_ada, b_ada, norm1_g, w_in, q_norm_a, k_norm_a,
           sink_a, q_norm_b, k_norm_b, rel_bias, w_out, norm2_g, w1, w2):
    nb_p, nb_s = x_prompt.shape[0], x_sample.shape[0]
    c_all = jnp.concatenate([c_prompt, c_sample], axis=0)
    c_pad = jnp.zeros((ADA_ROWS, D_MODEL), jnp.float32).at[:nb_p + nb_s].set(c_all)
    mod = _ada(c_pad, w_ada[0], b_ada[0][None, :])
    mod_p = mod[:nb_p].reshape(nb_p, 6, D_MODEL)
    mod_s = mod[nb_p:nb_p + nb_s].reshape(nb_s, 6, D_MODEL)

    bias_a = _bias_tiles(rel_bias, jnp.asarray(_bias_codes(A_TQ, A_RADIUS, A_RADIUS, 1)), 0, A_HEADS)
    bias_b = [_bias_tiles(rel_bias, jnp.asarray(_bias_codes(B_TQ, B_RADIUS, B_RADIUS, r)),
                          A_HEADS + g * B_HEADS, B_HEADS)
              for g, r in enumerate(DILATIONS)]

    wts = _prep_weights(w_in[0], q_norm_a[0], k_norm_a[0], q_norm_b[0], k_norm_b[0],
                        w_out[0], w1[0], w2[0])
    y_p = _layer(x_prompt, mod_p, norm1_g, norm2_g, sink_a[0], wts, bias_a, bias_b)
    y_s = _layer(x_sample, mod_s, norm1_g, norm2_g, sink_a[0], wts, bias_a, bias_b)
    return (y_p, y_s)
```

```python
import functools
import math

import numpy as np
import jax
import jax.numpy as jnp
from jax import lax
from jax.experimental import pallas as pl
from jax.experimental.pallas import tpu as pltpu

D_MODEL = 2048
HEAD_DIM = 128
A_HEADS = 8
A_KV_HEADS = 2
A_GROUP = A_HEADS // A_KV_HEADS
A_RADIUS = 128
B_HEADS = 8
B_RADIUS = 64
DILATIONS = (1, 4, 16)
NUM_BUCKETS = 32
REL_MAX_DISTANCE = 1024
D_FF = 4 * D_MODEL
A_Q_COLS = A_HEADS * HEAD_DIM
A_KV_COLS = A_KV_HEADS * HEAD_DIM
A_COLS = A_Q_COLS + 2 * A_KV_COLS
B_GROUP_COLS = B_HEADS * HEAD_DIM
B_QKV_COLS = 3 * B_GROUP_COLS
PROJ_COLS = A_COLS + len(DILATIONS) * B_QKV_COLS
A_COL0 = len(DILATIONS) * B_QKV_COLS
EPS = 1e-6
NEG_INF = -1e30
LOG2E = math.log2(math.e)
LN2 = math.log(2.0)
SCALE_LOG2E = HEAD_DIM ** -0.5 * LOG2E

ADA_ROWS = 16
ADA_TN = 1024
TM = 1024
CHUNK = B_RADIUS
PROJ_TN = 768
PROJ_NT = PROJ_COLS // PROJ_TN
PROJ_SLABS = PROJ_TN // HEAD_DIM
A_TQ = 128
A_TILE = 512
B_KEYS = 256
LSE_LANES = 128
MIX_TM = 256
FFN_TF = 1024
VMEM_LIMIT = 56 * 1024 * 1024


def _cparams(n_axes):
    return pltpu.CompilerParams(
        dimension_semantics=("arbitrary",) * n_axes, vmem_limit_bytes=VMEM_LIMIT)


def _ada_kernel(c_ref, w_ref, b_ref, o_ref):
    c = c_ref[...]
    s = c * (1.0 / (1.0 + jnp.exp(-c)))
    acc = jnp.dot(s.astype(jnp.bfloat16), w_ref[...].astype(jnp.bfloat16),
                  preferred_element_type=jnp.float32)
    o_ref[...] = acc + b_ref[...]


def _ada(c_pad, w_ada, b_ada):
    n = w_ada.shape[1]
    return pl.pallas_call(
        _ada_kernel,
        out_shape=jax.ShapeDtypeStruct((ADA_ROWS, n), jnp.float32),
        grid=(n // ADA_TN,),
        in_specs=[pl.BlockSpec((ADA_ROWS, D_MODEL), lambda j: (0, 0)),
                  pl.BlockSpec((D_MODEL, ADA_TN), lambda j: (0, j)),
                  pl.BlockSpec((1, ADA_TN), lambda j: (0, j))],
        out_specs=pl.BlockSpec((ADA_ROWS, ADA_TN), lambda j: (0, j)),
        compiler_params=_cparams(1),
        name="ada",
    )(c_pad, w_ada, b_ada)


def _rel_bucket_np(rel):
    half = NUM_BUCKETS // 2
    exact = half // 2
    n = np.abs(rel)
    large = exact + (np.log(np.maximum(n, 1).astype(np.float32) / np.float32(exact))
                     / np.float32(math.log(REL_MAX_DISTANCE / exact)) * (half - exact)).astype(np.int32)
    large = np.minimum(large, half - 1)
    return np.where(rel > 0, half, 0) + np.where(n < exact, n, large)


def _bias_codes(tq, halo, radius, dilation, tk):
    col = np.arange(tk)[None, :]
    delta = col - halo - np.arange(tq)[:, None]
    ok = (np.abs(delta) <= radius) & (col < tq + 2 * halo)
    code = np.where(ok, _rel_bucket_np(delta * dilation), -1).astype(np.int32)
    out = []
    for v in range(4):
        c = code
        if v & 1:
            c = np.where(col < halo, -1, c)
        if v & 2:
            c = np.where(col >= halo + tq, -1, c)
        out.append(c)
    return np.stack(out).astype(np.int32)


def _bias_kernel(rb_ref, code_ref, o_ref, *, head0, n_heads):
    code = code_ref[0]
    for h in range(n_heads):
        acc = jnp.full(code.shape, NEG_INF, jnp.float32)
        for b in range(NUM_BUCKETS):
            acc = jnp.where(code == b, rb_ref[b, head0 + h] * LOG2E, acc)
        o_ref[0, h] = acc


def _bias_tiles(rel_bias, codes, head0, n_heads):
    _, tq, tk = codes.shape
    return pl.pallas_call(
        functools.partial(_bias_kernel, head0=head0, n_heads=n_heads),
        out_shape=jax.ShapeDtypeStruct((4, n_heads, tq, tk), jnp.float32),
        grid=(4,),
        in_specs=[pl.BlockSpec(memory_space=pltpu.SMEM),
                  pl.BlockSpec((1, tq, tk), lambda v: (v, 0, 0))],
        out_specs=pl.BlockSpec((1, n_heads, tq, tk), lambda v: (v, 0, 0, 0)),
        compiler_params=_cparams(1),
        name="bias_tiles",
    )(rel_bias, codes)


def _tile_dilation(jt):
    g = jt // (B_QKV_COLS // PROJ_TN)
    return DILATIONS[g] if g < len(DILATIONS) else 1


def _proj_kernel(x_ref, mod_ref, g1_ref, w_ref, gain_ref, flag_ref, o_ref, h_ref, acc_ref, tmp_ref, *, n_steps):
    s = pl.program_id(0)
    jt = s % PROJ_NT
    jp = jnp.maximum(s - 1, 0) % PROJ_NT

    @pl.when(s == 0)
    def _():
        acc_ref[...] = jnp.zeros_like(acc_ref)

    @pl.when((jt == 0) & (s < n_steps - 1))
    def _():
        x = x_ref[0]
        inv = lax.rsqrt(jnp.mean(x * x, axis=-1, keepdims=True) + EPS)
        h = (x * inv) * g1_ref[...]
        h = h * (1.0 + mod_ref[0, 1:2, :]) + mod_ref[0, 0:1, :]
        h_ref[...] = h.astype(jnp.bfloat16)

    def step(r):
        n = TM // r
        for k in range(PROJ_SLABS):
            ks = slice(k * HEAD_DIM, (k + 1) * HEAD_DIM)
            gain = gain_ref[0, :, ks]
            keep = flag_ref[0, :, ks] > 0.0

            def emit(a, c):
                inv = lax.rsqrt(jnp.mean(a * a, axis=-1, keepdims=True) + EPS)
                res = jnp.where(keep, (a * inv) * gain, a)
                o_ref[0, c * n:(c + 1) * n, ks] = res.astype(jnp.bfloat16)

            if r == 1:
                emit(acc_ref[k], 0)
            elif r == 4:
                for c in range(r):
                    emit(acc_ref[k, pl.ds(c, n, stride=r), :], c)
            else:
                for c4 in range(4):
                    tmp_ref[c4] = acc_ref[k, pl.ds(c4, TM // 4, stride=4), :]
                    for j in range(r // 4):
                        emit(tmp_ref[c4, pl.ds(j, n, stride=4), :], 4 * j + c4)
        res = jnp.dot(h_ref[...], w_ref[...], preferred_element_type=jnp.float32)
        for k in range(PROJ_SLABS):
            acc_ref[k] = res[:, k * HEAD_DIM:(k + 1) * HEAD_DIM]

    for r in DILATIONS:
        hit = functools.reduce(jnp.logical_or,
                               [jp == t for t in range(PROJ_NT) if _tile_dilation(t) == r])
        pl.when(hit)(functools.partial(step, r))


def _proj(x_tiles, mod, norm1_g, w_in_p, gain_tiles, flag_tiles, tiles_per_batch):
    nt = x_tiles.shape[0]
    n_steps = nt * PROJ_NT + 1
    last = n_steps - 2

    def tile_of(s):
        return jnp.minimum(s, last) // PROJ_NT

    def prev(s):
        return jnp.maximum(s - 1, 0)

    return pl.pallas_call(
        functools.partial(_proj_kernel, n_steps=n_steps),
        out_shape=jax.ShapeDtypeStruct((nt, TM, PROJ_COLS), jnp.bfloat16),
        grid=(n_steps,),
        in_specs=[pl.BlockSpec((1, TM, D_MODEL), lambda s: (tile_of(s), 0, 0)),
                  pl.BlockSpec((1, 6, D_MODEL), lambda s: (tile_of(s) // tiles_per_batch, 0, 0)),
                  pl.BlockSpec((1, D_MODEL), lambda s: (0, 0)),
                  pl.BlockSpec((D_MODEL, PROJ_TN), lambda s: (0, jnp.minimum(s, last) % PROJ_NT)),
                  pl.BlockSpec((1, 1, PROJ_TN), lambda s: (prev(s) % PROJ_NT, 0, 0)),
                  pl.BlockSpec((1, 1, PROJ_TN), lambda s: (prev(s) % PROJ_NT, 0, 0))],
        out_specs=pl.BlockSpec((1, TM, PROJ_TN), lambda s: (prev(s) // PROJ_NT, 0, prev(s) % PROJ_NT)),
        scratch_shapes=[pltpu.VMEM((TM, D_MODEL), jnp.bfloat16),
                        pltpu.VMEM((PROJ_SLABS, TM, HEAD_DIM), jnp.float32),
                        pltpu.VMEM((4, TM // 4, HEAD_DIM), jnp.float32)],
        compiler_params=_cparams(1),
        name="proj",
    )(x_tiles, mod, norm1_g, w_in_p, gain_tiles, flag_tiles)


def _attn_a_kernel(sink_ref, q_ref, kp_ref, kc_ref, kn_ref, vp_ref, vc_ref, vn_ref, bias_ref, o_ref,
                   *, n_blocks, n_tiles):
    tq = A_TQ
    i = pl.program_id(1)
    first = (i == 0).astype(jnp.int32)
    last = (i == n_tiles - 1).astype(jnp.int32)
    def window(p_ref, c_ref, n_ref, jb, cols):
        left = p_ref[0, :, cols] if jb == 0 else c_ref[0, (jb - 1) * tq:jb * tq, cols]
        right = n_ref[0, :, cols] if jb == n_blocks - 1 else c_ref[0, (jb + 1) * tq:(jb + 2) * tq, cols]
        return jnp.concatenate([left, c_ref[0, jb * tq:(jb + 1) * tq, cols], right], axis=0)

    items = [(jb, kh) for jb in range(n_blocks) for kh in range(A_KV_HEADS)]
    scores = []
    for jb, kh in items:
        k = window(kp_ref, kc_ref, kn_ref, jb, slice(kh * HEAD_DIM, (kh + 1) * HEAD_DIM))
        q = jnp.concatenate(
            [q_ref[0, jb * tq:(jb + 1) * tq, (kh * A_GROUP + g) * HEAD_DIM:(kh * A_GROUP + g + 1) * HEAD_DIM]
             for g in range(A_GROUP)], axis=0)
        scores.append(lax.dot_general(q, k, (((1,), (1,)), ((), ())), preferred_element_type=jnp.float32))
    probs = []
    for (jb, kh), s in zip(items, scores):
        var = (first if jb == 0 else 0) + (2 * last if jb == n_blocks - 1 else 0)
        ps, ls = [], []
        for g in range(A_GROUP):
            h = kh * A_GROUP + g
            sg = s[g * tq:(g + 1) * tq, :] * SCALE_LOG2E + bias_ref[var, h]
            sink = sink_ref[h] * LOG2E
            m = jnp.maximum(jnp.max(sg, axis=-1, keepdims=True), sink)
            p = jnp.exp2(sg - m)
            ls.append(jnp.sum(p, axis=-1, keepdims=True) + jnp.exp2(sink - m))
            ps.append(p.astype(jnp.bfloat16))
        probs.append((jnp.concatenate(ps, axis=0), ls))
    for (jb, kh), (p, ls) in zip(items, probs):
        v = window(vp_ref, vc_ref, vn_ref, jb, slice(kh * HEAD_DIM, (kh + 1) * HEAD_DIM))
        o = jnp.dot(p, v, preferred_element_type=jnp.float32)
        for g in range(A_GROUP):
            h = kh * A_GROUP + g
            o_ref[0, jb * tq:(jb + 1) * tq, h * HEAD_DIM:(h + 1) * HEAD_DIM] = (
                o[g * tq:(g + 1) * tq, :] / ls[g]).astype(jnp.bfloat16)


def _attn_a(proj, sink, bias_a):
    B, S, _ = proj.shape
    tile = min(A_TILE, S)
    n = S // tile
    per = tile // A_TQ
    nh = S // A_TQ
    qcol = A_COL0 // A_Q_COLS
    kcol = (A_COL0 + A_Q_COLS) // A_KV_COLS
    vcol = kcol + 1

    def cur(col):
        return pl.BlockSpec((1, tile, A_KV_COLS), lambda b, i: (b, i, col))

    def before(col):
        return pl.BlockSpec((1, A_TQ, A_KV_COLS), lambda b, i: (b, jnp.maximum(i * per - 1, 0), col))

    def after(col):
        return pl.BlockSpec((1, A_TQ, A_KV_COLS), lambda b, i: (b, jnp.minimum((i + 1) * per, nh - 1), col))

    return pl.pallas_call(
        functools.partial(_attn_a_kernel, n_blocks=per, n_tiles=n),
        out_shape=jax.ShapeDtypeStruct((B, S, A_Q_COLS), jnp.bfloat16),
        grid=(B, n),
        in_specs=[pl.BlockSpec(memory_space=pltpu.SMEM),
                  pl.BlockSpec((1, tile, A_Q_COLS), lambda b, i: (b, i, qcol)),
                  before(kcol), cur(kcol), after(kcol),
                  before(vcol), cur(vcol), after(vcol),
                  pl.BlockSpec((4, A_HEADS, A_TQ, 3 * A_TQ), lambda b, i: (0, 0, 0, 0))],
        out_specs=pl.BlockSpec((1, tile, A_Q_COLS), lambda b, i: (b, i, 0)),
        compiler_params=_cparams(2),
        name="attn_a",
    )(sink, proj, proj, proj, proj, proj, proj, proj, bias_a)


def _attn_b_kernel(q_ref, kp_ref, kc_ref, kn_ref, vp_ref, vc_ref, vn_ref, bias_ref, o_ref, lse_ref,
                   *, r, n_tiles):
    nsub = TM // (r * CHUNK)
    nq = 2 if nsub >= 2 else 1
    tq = nq * CHUNK
    nblk = nsub // nq
    i = pl.program_id(1)
    first = (i == 0).astype(jnp.int32)
    last = (i == n_tiles - 1).astype(jnp.int32)
    lane = lax.broadcasted_iota(jnp.int32, (tq, LSE_LANES), 1)

    def window(p_ref, c_ref, n_ref, blk, hs):
        c, sb0, from_before, to_after = blk
        left = p_ref[0, c, 0, :, hs] if from_before else c_ref[0, c, sb0 - 1, :, hs]
        right = n_ref[0, c, 0, :, hs] if to_after else c_ref[0, c, sb0 + nq, :, hs]
        mid = c_ref[0, c, pl.ds(sb0, nq), :, hs].reshape(tq, HEAD_DIM)
        parts = [left, mid, right] + [right] * ((B_KEYS - tq - 2 * CHUNK) // CHUNK)
        return jnp.concatenate(parts, axis=0)

    def group(blocks):
        items = [(blk, h) for blk in blocks for h in range(B_HEADS)]
        scores = []
        for blk, h in items:
            hs = slice(h * HEAD_DIM, (h + 1) * HEAD_DIM)
            q = q_ref[0, blk[0], pl.ds(blk[1], nq), :, hs].reshape(tq, HEAD_DIM)
            k = window(kp_ref, kc_ref, kn_ref, blk, hs)
            scores.append(lax.dot_general(q, k, (((1,), (1,)), ((), ())),
                                          preferred_element_type=jnp.float32))
        probs = []
        for (blk, h), s in zip(items, scores):
            var = (first if blk[2] else 0) + (2 * last if blk[3] else 0)
            s = s * SCALE_LOG2E + bias_ref[var, h]
            m = jnp.max(s, axis=-1, keepdims=True)
            p = jnp.exp2(s - m)
            probs.append((p.astype(jnp.bfloat16), m, jnp.sum(p, axis=-1, keepdims=True)))
        lse_tiles = [jnp.zeros((tq, LSE_LANES), jnp.float32) for _ in blocks]
        for n, ((blk, h), (p, m, l)) in enumerate(zip(items, probs)):
            hs = slice(h * HEAD_DIM, (h + 1) * HEAD_DIM)
            v = window(vp_ref, vc_ref, vn_ref, blk, hs)
            o = jnp.dot(p, v, preferred_element_type=jnp.float32)
            o_ref[0, blk[0], pl.ds(blk[1], nq), :, hs] = (o / l).reshape(nq, CHUNK, HEAD_DIM)
            lse_tiles[n // B_HEADS] = jnp.where(lane == h, (m + jnp.log2(l)) * LN2, lse_tiles[n // B_HEADS])
        for blk, tile in zip(blocks, lse_tiles):
            lse_ref[0, blk[0], pl.ds(blk[1], nq)] = tile.reshape(nq, CHUNK, LSE_LANES)

    def loop(n, fn):
        def body(t, carry):
            fn(t)
            return carry
        lax.fori_loop(0, n, body, 0)

    if nblk == 1:
        loop(r // 2, lambda t: group([(2 * t, 0, True, True), (2 * t + 1, 0, True, True)]))
    else:
        def residue(c):
            group([(c, 0, True, False), (c, nq, False, nblk == 2)])
            if nblk > 4:
                loop(nblk // 2 - 2, lambda t: group([(c, (2 * t + 2) * nq, False, False),
                                                     (c, (2 * t + 3) * nq, False, False)]))
            if nblk > 2:
                group([(c, (nblk - 2) * nq, False, False), (c, (nblk - 1) * nq, False, True)])
        if r == 1:
            residue(0)
        else:
            loop(r, residue)


def _attn_b(proj5, bias_g, r, g, n_batch):
    nt_all, _, nsub, _, _ = proj5.shape
    n = nt_all // n_batch
    qcol = g * B_QKV_COLS // B_GROUP_COLS
    kcol, vcol = qcol + 1, qcol + 2
    tq = bias_g.shape[2]

    def cur(col):
        return pl.BlockSpec((1, r, nsub, CHUNK, B_GROUP_COLS), lambda b, i: (b * n + i, 0, 0, 0, col))

    def before(col):
        return pl.BlockSpec((1, r, 1, CHUNK, B_GROUP_COLS),
                            lambda b, i: (b * n + jnp.maximum(i - 1, 0), 0, nsub - 1, 0, col))

    def after(col):
        return pl.BlockSpec((1, r, 1, CHUNK, B_GROUP_COLS),
                            lambda b, i: (b * n + jnp.minimum(i + 1, n - 1), 0, 0, 0, col))

    return pl.pallas_call(
        functools.partial(_attn_b_kernel, r=r, n_tiles=n),
        out_shape=[jax.ShapeDtypeStruct((nt_all, r, nsub, CHUNK, B_GROUP_COLS), jnp.float32),
                   jax.ShapeDtypeStruct((nt_all, r, nsub, CHUNK, LSE_LANES), jnp.float32)],
        grid=(n_batch, n),
        in_specs=[cur(qcol), before(kcol), cur(kcol), after(kcol), before(vcol), cur(vcol), after(vcol),
                  pl.BlockSpec((4, B_HEADS, tq, B_KEYS), lambda b, i: (0, 0, 0, 0))],
        out_specs=[pl.BlockSpec((1, r, nsub, CHUNK, B_GROUP_COLS), lambda b, i: (b * n + i, 0, 0, 0, 0)),
                   pl.BlockSpec((1, r, nsub, CHUNK, LSE_LANES), lambda b, i: (b * n + i, 0, 0, 0, 0))],
        compiler_params=_cparams(2),
        name="attn_b",
    )(*([proj5] * 7), bias_g)


def _mix_kernel(x_ref, mod_ref, g2_ref, ya_ref, o0_ref, o1_ref, o2_ref, l0_ref, l1_ref, l2_ref, w_ref,
                x1_ref, h2_ref, os1_ref, os2_ref, ls1_ref, ls2_ref, ya0_ref, ya1_ref, acc_ref, *, tm):
    s = pl.program_id(0)

    @pl.when(s == 0)
    def _():
        ya0_ref[...] = jnp.zeros_like(ya0_ref)
        ya1_ref[...] = jnp.zeros_like(ya1_ref)
        acc_ref[...] = jnp.zeros_like(acc_ref)

    def step(y_in_ref, y_out_ref):
        x1 = x_ref[0] + mod_ref[0, 2:3, :] * acc_ref[...]
        x1_ref[0] = x1
        inv = lax.rsqrt(jnp.mean(x1 * x1, axis=-1, keepdims=True) + EPS)
        h2 = ((x1 * inv) * g2_ref[...]) * (1.0 + mod_ref[0, 4:5, :]) + mod_ref[0, 3:4, :]
        h2_ref[0] = h2.astype(jnp.bfloat16)
        acc_ref[...] = jnp.dot(y_in_ref[...], w_ref[...], preferred_element_type=jnp.float32)
        for r, o_ref, l_ref, os_ref, ls_ref in ((4, o1_ref, l1_ref, os1_ref, ls1_ref),
                                                (16, o2_ref, l2_ref, os2_ref, ls2_ref)):
            n = tm // r
            for c in range(r):
                ls_ref[pl.ds(c, n, stride=r), :] = l_ref[0, c]
                for h in range(B_HEADS):
                    os_ref[h, pl.ds(c, n, stride=r), :] = o_ref[0, c, :, h * HEAD_DIM:(h + 1) * HEAD_DIM]
        lses = (l0_ref[0, 0], ls1_ref[...], ls2_ref[...])
        m = jnp.maximum(jnp.maximum(lses[0], lses[1]), lses[2])
        es = [jnp.exp(l - m) for l in lses]
        tot = es[0] + es[1] + es[2]
        alphas = [e / tot for e in es]
        y_out_ref[:, 0:A_Q_COLS] = ya_ref[0]
        for h in range(B_HEADS):
            hs = slice(h * HEAD_DIM, (h + 1) * HEAD_DIM)
            yb = (alphas[0][:, h:h + 1] * o0_ref[0, 0, :, hs]
                  + alphas[1][:, h:h + 1] * os1_ref[h]
                  + alphas[2][:, h:h + 1] * os2_ref[h])
            y_out_ref[:, A_Q_COLS + h * HEAD_DIM:A_Q_COLS + (h + 1) * HEAD_DIM] = yb.astype(jnp.bfloat16)

    pl.when(s % 2 == 0)(functools.partial(step, ya1_ref, ya0_ref))
    pl.when(s % 2 == 1)(functools.partial(step, ya0_ref, ya1_ref))


def _mix(x_tiles, mod, norm2_g, ya_tiles, outs, lses, w_out_b, tiles_per_batch):
    nt, tm, _ = x_tiles.shape
    per = TM // tm
    n_steps = nt + 2

    def cur(s):
        return jnp.minimum(s, nt - 1)

    def done(s):
        return jnp.clip(s - 2, 0, nt - 1)

    def strided_spec(r, cols):
        return pl.BlockSpec((1, r, tm // r, cols), lambda s: (cur(s) // per, 0, cur(s) % per, 0))

    in_specs = [pl.BlockSpec((1, tm, D_MODEL), lambda s: (done(s), 0, 0)),
                pl.BlockSpec((1, 6, D_MODEL), lambda s: (done(s) // tiles_per_batch, 0, 0)),
                pl.BlockSpec((1, D_MODEL), lambda s: (0, 0)),
                pl.BlockSpec((1, tm, A_Q_COLS), lambda s: (cur(s), 0, 0))]
    in_specs += [strided_spec(r, B_GROUP_COLS) for r in DILATIONS]
    in_specs += [strided_spec(r, LSE_LANES) for r in DILATIONS]
    in_specs += [pl.BlockSpec((D_MODEL, D_MODEL), lambda s: (0, 0))]
    return pl.pallas_call(
        functools.partial(_mix_kernel, tm=tm),
        out_shape=[jax.ShapeDtypeStruct((nt, tm, D_MODEL), jnp.float32),
                   jax.ShapeDtypeStruct((nt, tm, D_MODEL), jnp.bfloat16)],
        grid=(n_steps,),
        in_specs=in_specs,
        out_specs=[pl.BlockSpec((1, tm, D_MODEL), lambda s: (done(s), 0, 0)),
                   pl.BlockSpec((1, tm, D_MODEL), lambda s: (done(s), 0, 0))],
        scratch_shapes=[pltpu.VMEM((B_HEADS, tm, HEAD_DIM), jnp.float32),
                        pltpu.VMEM((B_HEADS, tm, HEAD_DIM), jnp.float32),
                        pltpu.VMEM((tm, LSE_LANES), jnp.float32),
                        pltpu.VMEM((tm, LSE_LANES), jnp.float32),
                        pltpu.VMEM((tm, D_MODEL), jnp.bfloat16),
                        pltpu.VMEM((tm, D_MODEL), jnp.bfloat16),
                        pltpu.VMEM((tm, D_MODEL), jnp.float32)],
        compiler_params=_cparams(1),
        name="mix",
    )(x_tiles, mod, norm2_g, ya_tiles, *outs, *lses, w_out_b)


def _ffn_kernel(x1_ref, h2_ref, mod_ref, w1_ref, w2_ref, o_ref, acc_ref):
    f = pl.program_id(2)

    @pl.when(f == 0)
    def _():
        acc_ref[...] = jnp.zeros_like(acc_ref)

    u = jnp.dot(h2_ref[0], w1_ref[...], preferred_element_type=jnp.float32)
    u = jnp.maximum(u, 0.0)
    u = (u * u).astype(jnp.bfloat16)
    acc_ref[...] += jnp.dot(u, w2_ref[...], preferred_element_type=jnp.float32)

    @pl.when(f == pl.num_programs(2) - 1)
    def _():
        o_ref[0] = x1_ref[0] + mod_ref[0, 5:6, :] * acc_ref[...]


def _ffn(x1, h2, mod, w1_b, w2_b, tm):
    B, S, _ = x1.shape
    tf = FFN_TF
    return pl.pallas_call(
        _ffn_kernel,
        out_shape=jax.ShapeDtypeStruct((B, S, D_MODEL), jnp.float32),
        grid=(B, S // tm, D_FF // tf),
        in_specs=[pl.BlockSpec((1, tm, D_MODEL), lambda b, i, f: (b, i, 0)),
                  pl.BlockSpec((1, tm, D_MODEL), lambda b, i, f: (b, i, 0)),
                  pl.BlockSpec((1, 6, D_MODEL), lambda b, i, f: (b, 0, 0)),
                  pl.BlockSpec((D_MODEL, tf), lambda b, i, f: (0, f)),
                  pl.BlockSpec((tf, D_MODEL), lambda b, i, f: (f, 0))],
        out_specs=pl.BlockSpec((1, tm, D_MODEL), lambda b, i, f: (b, i, 0)),
        scratch_shapes=[pltpu.VMEM((tm, D_MODEL), jnp.float32)],
        compiler_params=_cparams(3),
        name="ffn",
    )(x1, h2, mod, w1_b, w2_b)


def _prep_weights(w_in, q_norm_a, k_norm_a, q_norm_b, k_norm_b, w_out, w1, w2):
    qb0 = A_COLS
    kb0 = qb0 + len(DILATIONS) * B_GROUP_COLS
    vb0 = kb0 + len(DILATIONS) * B_GROUP_COLS
    ones_h = jnp.ones((HEAD_DIM,), jnp.float32)
    parts, gains, flags = [], [], []
    for g in range(len(DILATIONS)):
        for base in (qb0, kb0, vb0):
            parts.append(w_in[:, base + g * B_GROUP_COLS: base + (g + 1) * B_GROUP_COLS])
        gains += [jnp.tile(q_norm_b[g], B_HEADS), jnp.tile(k_norm_b[g], B_HEADS), jnp.tile(ones_h, B_HEADS)]
        flags += [np.ones(2 * B_GROUP_COLS, np.float32), np.zeros(B_GROUP_COLS, np.float32)]
    parts.append(w_in[:, :A_COLS])
    gains += [jnp.tile(q_norm_a, A_HEADS), jnp.tile(k_norm_a, A_KV_HEADS), jnp.tile(ones_h, A_KV_HEADS)]
    flags += [np.ones(A_Q_COLS + A_KV_COLS, np.float32), np.zeros(A_KV_COLS, np.float32)]
    w_in_p = jnp.concatenate(parts, axis=1).astype(jnp.bfloat16)
    gain_tiles = jnp.concatenate(gains).astype(jnp.float32).reshape(PROJ_NT, 1, PROJ_TN)
    flag_tiles = jnp.asarray(np.concatenate(flags)).reshape(PROJ_NT, 1, PROJ_TN)
    return (w_in_p, gain_tiles, flag_tiles, w_out.astype(jnp.bfloat16),
            w1.astype(jnp.bfloat16), w2.astype(jnp.bfloat16))


def _pick_tile(S, want, mult):
    t = min(want, S)
    while S % t or t % mult:
        t //= 2
    return t


def _layer(x, mod, norm1_g, norm2_g, sink, wts, bias_a, bias_b):
    w_in_p, gain_tiles, flag_tiles, w_out_b, w1_b, w2_b = wts
    B, S, _ = x.shape
    nt = B * S // TM
    proj = _proj(x.reshape(nt, TM, D_MODEL), mod, norm1_g, w_in_p, gain_tiles, flag_tiles, S // TM)
    ya = _attn_a(proj.reshape(B, S, PROJ_COLS), sink, bias_a)
    outs, lses = [], []
    for g, r in enumerate(DILATIONS):
        nsub = TM // (r * CHUNK)
        o, lse = _attn_b(proj.reshape(nt, r, nsub, CHUNK, PROJ_COLS), bias_b[g], r, g, B)
        outs.append(o.reshape(nt, r, TM // r, B_GROUP_COLS))
        lses.append(lse.reshape(nt, r, TM // r, LSE_LANES))
    nt3 = B * S // MIX_TM
    x1, h2 = _mix(x.reshape(nt3, MIX_TM, D_MODEL), mod, norm2_g, ya.reshape(nt3, MIX_TM, A_Q_COLS),
                  outs, lses, w_out_b, S // MIX_TM)
    return _ffn(x1.reshape(B, S, D_MODEL), h2.reshape(B, S, D_MODEL), mod, w1_b, w2_b,
                _pick_tile(S, 512, 128))


def kernel(x_prompt, x_sample, c_prompt, c_sample, w_ada, b_ada, norm1_g, w_in, q_norm_a, k_norm_a,
           sink_a, q_norm_b, k_norm_b, rel_bias, w_out, norm2_g, w1, w2):
    nb_p, nb_s = x_prompt.shape[0], x_sample.shape[0]
    c_all = jnp.concatenate([c_prompt, c_sample], axis=0)
    c_pad = jnp.zeros((ADA_ROWS, D_MODEL), jnp.float32).at[:nb_p + nb_s].set(c_all)
    mod = _ada(c_pad, w_ada[0], b_ada[0][None, :])
    mod_p = mod[:nb_p].reshape(nb_p, 6, D_MODEL)
    mod_s = mod[nb_p:nb_p + nb_s].reshape(nb_s, 6, D_MODEL)

    bias_a = _bias_tiles(rel_bias, jnp.asarray(_bias_codes(A_TQ, A_RADIUS, A_RADIUS, 1, 3 * A_TQ)),
                         0, A_HEADS)
    bias_b = []
    for g, r in enumerate(DILATIONS):
        tq = min(2 * CHUNK, TM // r)
        bias_b.append(_bias_tiles(rel_bias, jnp.asarray(_bias_codes(tq, B_RADIUS, B_RADIUS, r, B_KEYS)),
                                  A_HEADS + g * B_HEADS, B_HEADS))

    wts = _prep_weights(w_in[0], q_norm_a[0], k_norm_a[0], q_norm_b[0], k_norm_b[0],
                        w_out[0], w1[0], w2[0])
    y_p = _layer(x_prompt, mod_p, norm1_g, norm2_g, sink_a[0], wts, bias_a, bias_b)
    y_s = _layer(x_sample, mod_s, norm1_g, norm2_g, sink_a[0], wts, bias_a, bias_b)
    return (y_p, y_s)
```
